```python
import jax, jax.numpy as jnp
from jax import lax
import numpy as np

D_MODEL = 1024
BATCH = 2
SEQ = 16384
DEPTH = 4

N_HEADS_A = 8
HEAD_DIM = 64
D_ATTN = N_HEADS_A * HEAD_DIM
DILATED_BRANCHES = ((128, 1), (512, 4), (2048, 16))
MAX_WINDOW = 2048
Q_BLOCK = 128
POOL_WINDOWS = (2, 4, 8, 16)
N_POOL_GROUPS = 4
D_POOL = D_MODEL // 2
POOL_GROUP_DIM = D_POOL // N_POOL_GROUPS
D_MIX_EVEN = D_ATTN + D_POOL
D_IN_EVEN = 3 * D_ATTN + D_POOL
D_CONV = D_MODEL
CONV_WIDTH = 3
PEER_HEADS = 8
PEER_N_KEYS = 128
PEER_N_EXPERTS = PEER_N_KEYS * PEER_N_KEYS
PEER_TOPK = 16
PEER_D_KEY = 256
PEER_HALF = PEER_D_KEY // 2
TOKEN_CHUNK = 128
N_EVEN = (DEPTH + 1) // 2
N_ODD = DEPTH // 2
EPS = 1e-6

kernel_name = "hybrid_dilated_pool_conv_peer_trunk"


def rms_norm(x, g):
    x32 = x.astype(jnp.float32)
    y = x32 * lax.rsqrt(jnp.mean(x32 * x32, axis=-1, keepdims=True) + EPS)
    return (y * g.astype(jnp.float32)).astype(x.dtype)


def dilated_sliding_attention(q, k, v):
    b, s, h, hd = q.shape
    n_blocks = s // Q_BLOCK
    pad = ((0, 0), (MAX_WINDOW, 0), (0, 0), (0, 0))
    kp = jnp.pad(k, pad)
    vp = jnp.pad(v, pad)
    qi = jnp.arange(Q_BLOCK)
    scale = HEAD_DIM ** -0.5

    def block(i):
        s0 = i * Q_BLOCK
        qb = lax.dynamic_slice_in_dim(q, s0, Q_BLOCK, axis=1).astype(jnp.float32)
        outs, lses = [], []
        for window, dil in DILATED_BRANCHES:
            j = jnp.arange(window // dil + 1)
            pos = s0 + qi[:, None] - dil * j[None, :]
            idx = pos + MAX_WINDOW
            kg = kp[:, idx].astype(jnp.float32)
            vg = vp[:, idx].astype(jnp.float32)
            sc = jnp.einsum('bqhd,bqjhd->bqhj', qb, kg) * scale
            sc = jnp.where((pos >= 0)[None, :, None, :], sc, -jnp.inf)
            m = jnp.max(sc, axis=-1, keepdims=True)
            p = jnp.exp(sc - m)
            den = jnp.sum(p, axis=-1, keepdims=True)
            outs.append(jnp.einsum('bqhj,bqjhd->bqhd', p, vg) / den)
            lses.append(m[..., 0] + jnp.log(den[..., 0]))
        w = jax.nn.softmax(jnp.stack(lses, axis=-1), axis=-1)
        o = jnp.sum(w[..., None] * jnp.stack(outs, axis=3), axis=3)
        return o.astype(v.dtype)

    out = lax.map(block, jnp.arange(n_blocks))
    return out.transpose(1, 0, 2, 3, 4).reshape(b, s, h * hd)


def multiscale_pool(u, pool_w, pool_scale):
    b, s, _ = u.shape
    ug = u.reshape(b, s, N_POOL_GROUPS, POOL_GROUP_DIM)
    cs0 = jnp.pad(jnp.cumsum(ug.astype(jnp.float32), axis=1), ((0, 0), (1, 0), (0, 0), (0, 0)))
    t1 = jnp.arange(1, s + 1, dtype=jnp.float32)
    pooled = []
    for gi, w in enumerate(POOL_WINDOWS):
        cg = cs0[:, :, gi]
        upper = cg[:, 1:]
        lower = jnp.pad(cg, ((0, 0), (w - 1, 0), (0, 0)))[:, :s]
        cnt = jnp.minimum(t1, float(w))[None, :, None]
        pooled.append((upper - lower) / cnt)
    pooled = jnp.stack(pooled, axis=2)
    diff = (pooled - ug.astype(jnp.float32)).astype(u.dtype)
    y = jnp.einsum('bsgc,gcd->bsgd', diff, pool_w)
    return y.reshape(b, s, D_POOL) * pool_scale


def short_gated_conv(h, w_in, conv_w, w_out):
    s = h.shape[1]
    xin, gate_b, gate_c = jnp.split(h @ w_in, 3, axis=-1)
    u = gate_c * xin
    up = jnp.pad(u, ((0, 0), (CONV_WIDTH - 1, 0), (0, 0)))
    y = conv_w[0] * up[:, CONV_WIDTH - 1:CONV_WIDTH - 1 + s]
    for lag in range(1, CONV_WIDTH):
        y = y + conv_w[lag] * up[:, CONV_WIDTH - 1 - lag:CONV_WIDTH - 1 - lag + s]
    return (gate_b * y) @ w_out


def peer_ffn(h, w_q, key1, key2, u_tab, v_tab):
    b, s, d = h.shape
    hc = h.reshape((b * s) // TOKEN_CHUNK, TOKEN_CHUNK, d)

    def chunk(xc):
        q = (xc @ w_q).reshape(TOKEN_CHUNK, PEER_HEADS, PEER_D_KEY).astype(jnp.float32)
        s1 = jnp.einsum('thc,nc->thn', q[..., :PEER_HALF], key1.astype(jnp.float32))
        s2 = jnp.einsum('thc,nc->thn', q[..., PEER_HALF:], key2.astype(jnp.float32))
        v1, i1 = lax.top_k(s1, PEER_TOPK)
        v2, i2 = lax.top_k(s2, PEER_TOPK)
        cand = (v1[..., :, None] + v2[..., None, :]).reshape(TOKEN_CHUNK, PEER_HEADS, PEER_TOPK * PEER_TOPK)
        vs, ci = lax.top_k(cand, PEER_TOPK)
        e1 = jnp.take_along_axis(i1, ci // PEER_TOPK, axis=-1)
        e2 = jnp.take_along_axis(i2, ci % PEER_TOPK, axis=-1)
        expert = e1 * PEER_N_KEYS + e2
        g = jax.nn.softmax(vs, axis=-1)
        act = jax.nn.gelu(jnp.einsum('td,thkd->thk', xc, u_tab[expert]), approximate=False)
        return jnp.einsum('thk,thkd->td', (g * act).astype(xc.dtype), v_tab[expert])

    return lax.map(chunk, hc).reshape(b, s, d)


def setup_inputs(seed: int = 0) -> dict:
    key = jax.random.key(seed)
    ks = jax.random.split(key, 24)
    f32 = jnp.float32

    def nrm(k, shape, scale):
        return jax.random.normal(k, shape, f32) * scale

    D = D_MODEL
    return {
        "x": nrm(ks[0], (BATCH, SEQ, D), 1.0),
        "c": nrm(ks[1], (BATCH, D), 1.0),
        "ada_w": nrm(ks[2], (DEPTH, D, 6 * D), D ** -0.5),
        "ada_b": nrm(ks[3], (DEPTH, 6 * D), 0.02),
        "norm_mix_g": 1.0 + nrm(ks[4], (DEPTH, D), 0.02),
        "norm_ffn_g": 1.0 + nrm(ks[5], (DEPTH, D), 0.02),
        "ab_w_in": nrm(ks[6], (N_EVEN, D, D_IN_EVEN), D ** -0.5),
        "ab_q_gain": 1.0 + nrm(ks[7], (N_EVEN, N_HEADS_A, HEAD_DIM), 0.02),
        "ab_k_gain": 1.0 + nrm(ks[8], (N_EVEN, N_HEADS_A, HEAD_DIM), 0.02),
        "pool_w": nrm(ks[9], (N_EVEN, N_POOL_GROUPS, POOL_GROUP_DIM, POOL_GROUP_DIM), POOL_GROUP_DIM ** -0.5),
        "pool_scale": 1.0 + nrm(ks[10], (N_EVEN, D_POOL), 0.1),
        "ab_w_out": nrm(ks[11], (N_EVEN, D_MIX_EVEN, D), D_MIX_EVEN ** -0.5),
        "conv_w_in": nrm(ks[12], (N_ODD, D, 3 * D_CONV), D ** -0.5),
        "conv_w": nrm(ks[13], (N_ODD, CONV_WIDTH, D_CONV), CONV_WIDTH ** -0.5),
        "conv_w_out": nrm(ks[14], (N_ODD, D_CONV, D), D_CONV ** -0.5),
        "peer_w_q": nrm(ks[15], (DEPTH, D, PEER_HEADS * PEER_D_KEY), D ** -0.5),
        "peer_key1": nrm(ks[16], (DEPTH, PEER_N_KEYS, PEER_HALF), PEER_HALF ** -0.5),
        "peer_key2": nrm(ks[17], (DEPTH, PEER_N_KEYS, PEER_HALF), PEER_HALF ** -0.5),
        "peer_u": nrm(ks[18], (DEPTH, PEER_N_EXPERTS, D), D ** -0.5),
        "peer_v": nrm(ks[19], (DEPTH, PEER_N_EXPERTS, D), PEER_TOPK ** -0.5),
    }


def reference(x, c, ada_w, ada_b, norm_mix_g, norm_ffn_g, ab_w_in, ab_q_gain, ab_k_gain,
              pool_w, pool_scale, ab_w_out, conv_w_in, conv_w, conv_w_out,
              peer_w_q, peer_key1, peer_key2, peer_u, peer_v):
    b, s, _ = x.shape
    c_act = jax.nn.silu(c)
    for layer in range(DEPTH):
        mod = c_act @ ada_w[layer] + ada_b[layer]
        sh1, sc1, g1, sh2, sc2, g2 = [m[:, None, :] for m in jnp.split(mod, 6, axis=-1)]
        h = rms_norm(x, norm_mix_g[layer]) * (1.0 + sc1) + sh1
        if layer % 2 == 0:
            e = layer // 2
            proj = h @ ab_w_in[e]
            q, k, v, u = jnp.split(proj, [D_ATTN, 2 * D_ATTN, 3 * D_ATTN], axis=-1)
            q = rms_norm(q.reshape(b, s, N_HEADS_A, HEAD_DIM), ab_q_gain[e])
            k = rms_norm(k.reshape(b, s, N_HEADS_A, HEAD_DIM), ab_k_gain[e])
            v = v.reshape(b, s, N_HEADS_A, HEAD_DIM)
            y_attn = dilated_sliding_attention(q, k, v)
            y_pool = multiscale_pool(u, pool_w[e], pool_scale[e])
            y = jnp.concatenate([y_attn, y_pool], axis=-1) @ ab_w_out[e]
        else:
            o = layer // 2
            y = short_gated_conv(h, conv_w_in[o], conv_w[o], conv_w_out[o])
        x = x + g1 * y
        h = rms_norm(x, norm_ffn_g[layer]) * (1.0 + sc2) + sh2
        x = x + g2 * peer_ffn(h, peer_w_q[layer], peer_key1[layer], peer_key2[layer],
                              peer_u[layer], peer_v[layer])
    return x
```

```python
import functools
import math

import jax
import jax.numpy as jnp
from jax import lax
from jax.experimental import pallas as pl
from jax.experimental.pallas import tpu as pltpu

F32 = jnp.float32
BF16 = jnp.bfloat16
EPS = 1e-6

N_HEADS = 8
HEAD_DIM = 64
D_ATTN = N_HEADS * HEAD_DIM
ATTN_WINDOW_STEPS = 128
DILATIONS = (1, 4, 16)
SUPER = ATTN_WINDOW_STEPS * DILATIONS[-1]
POOL_WINDOWS = (2, 4, 8, 16)
POOL_GROUP = 128
POOL_HALO = 16
CONV_HALO = 8
PEER_HEADS = 8
PEER_KEYS = 128
PEER_TOPK = 16
PEER_HALF = 128
PEER_SLOTS = PEER_HEADS * PEER_TOPK
LANES = 128
VMEM_LIMIT = 56 * 1024 * 1024


def _cparams(*sem):
    return pltpu.CompilerParams(dimension_semantics=sem, vmem_limit_bytes=VMEM_LIMIT)


def _split_bf16(x):
    hi = x.astype(BF16)
    lo = (x - hi.astype(F32)).astype(BF16)
    return hi, lo


def _dot(a, b):
    return jnp.dot(a, b, preferred_element_type=F32)


def _dot_nt(a, b):
    return lax.dot_general(a, b, (((1,), (1,)), ((), ())), preferred_element_type=F32)


def _norm_mod(x, g, sc, sh):
    ms = jnp.mean(x * x, axis=-1, keepdims=True)
    y = x * lax.rsqrt(ms + EPS)
    return (y * g) * (1.0 + sc) + sh


def _ada_kernel(c_ref, w_ref, b_ref, o_ref):
    c = c_ref[...]
    a = c * jax.nn.sigmoid(c)
    ah, al = _split_bf16(a)
    wh, wl = _split_bf16(w_ref[...])
    o_ref[...] = _dot(ah, wh) + _dot(al, wh) + _dot(ah, wl) + b_ref[...]


def _ada_mod(c, ada_w, ada_b):
    depth, d, d6 = ada_w.shape
    b = c.shape[0]
    rows = 8
    c_pad = jnp.zeros((rows, d), F32).at[:b].set(c)
    tn = 768
    out = pl.pallas_call(
        _ada_kernel,
        grid=(depth, d6 // tn),
        in_specs=[
            pl.BlockSpec((rows, d), lambda l, j: (0, 0)),
            pl.BlockSpec((None, d, tn), lambda l, j: (l, 0, j)),
            pl.BlockSpec((None, 1, tn), lambda l, j: (l, 0, j)),
        ],
        out_specs=pl.BlockSpec((None, rows, tn), lambda l, j: (l, 0, j)),
        out_shape=jax.ShapeDtypeStruct((depth, rows, d6), F32),
        compiler_params=_cparams("parallel", "parallel"),
        name="ada_mod",
    )(c_pad, ada_w, ada_b.reshape(depth, 1, d6))
    return out[:, :b, :]


def _inproj_kernel(x_ref, g_ref, sc_ref, sh_ref, w_ref, qg_ref, kg_ref, hm_ref, o_ref):
    h = _norm_mod(x_ref[...], g_ref[...], sc_ref[...], sh_ref[...])
    proj = _dot(h.astype(BF16), w_ref[...])
    hm = hm_ref[...]

    def head_norm(z, gain, scale):
        sh_, sl_ = _split_bf16(z * z)
        ss = _dot(sh_, hm) + _dot(sl_, hm)
        return z * lax.rsqrt(ss * (1.0 / HEAD_DIM) + EPS) * (gain * scale)

    o_ref[:, 0:D_ATTN] = head_norm(proj[:, 0:D_ATTN], qg_ref[...], HEAD_DIM ** -0.5)
    o_ref[:, D_ATTN:2 * D_ATTN] = head_norm(proj[:, D_ATTN:2 * D_ATTN], kg_ref[...], 1.0)
    o_ref[:, 2 * D_ATTN:] = proj[:, 2 * D_ATTN:]


def _inproj(x2, g, sc, sh, w_bf16, q_gain, k_gain, seq, tm):
    n, d = x2.shape
    dout = w_bf16.shape[1]
    tiles_per_seq = seq // tm
    head_of_lane = jnp.arange(D_ATTN) // HEAD_DIM
    hm = (head_of_lane[:, None] == head_of_lane[None, :]).astype(BF16)
    bmap = lambda i: (i // tiles_per_seq, 0, 0)
    return pl.pallas_call(
        _inproj_kernel,
        grid=(n // tm,),
        in_specs=[
            pl.BlockSpec((tm, d), lambda i: (i, 0)),
            pl.BlockSpec((1, d), lambda i: (0, 0)),
            pl.BlockSpec((None, 1, d), bmap),
            pl.BlockSpec((None, 1, d), bmap),
            pl.BlockSpec((d, dout), lambda i: (0, 0)),
            pl.BlockSpec((1, D_ATTN), lambda i: (0, 0)),
            pl.BlockSpec((1, D_ATTN), lambda i: (0, 0)),
            pl.BlockSpec((D_ATTN, D_ATTN), lambda i: (0, 0)),
        ],
        out_specs=pl.BlockSpec((tm, dout), lambda i: (i, 0)),
        out_shape=jax.ShapeDtypeStruct((n, dout), F32),
        compiler_params=_cparams("parallel"),
        name="even_inproj",
    )(x2, g, sc, sh, w_bf16, q_gain, k_gain, hm)


def _attn_kernel(q_ref, kp_ref, kc_ref, vp_ref, vc_ref, o_ref, kfull, vfull, o_s, lse_s):
    w = ATTN_WINDOW_STEPS
    first = pl.program_id(2) == 0
    kfull[0:SUPER, :] = kp_ref[...]
    kfull[SUPER:2 * SUPER, :] = kc_ref[...]
    vfull[0:SUPER, :] = vp_ref[...]
    vfull[SUPER:2 * SUPER, :] = vc_ref[...]

    qi = lax.broadcasted_iota(jnp.int32, (w, 2 * w), 0)
    kj = lax.broadcasted_iota(jnp.int32, (w, 2 * w), 1)
    delta = qi + w - kj
    band = (delta >= 0) & (delta <= w)
    lane = lax.broadcasted_iota(jnp.int32, (w, LANES), 1)
    head0 = lane < HEAD_DIM

    def sub_block(br, dil, qstart, kstart):
        qs = q_ref[pl.ds(qstart, w, stride=dil), :]
        ks = kfull[pl.ds(kstart, 2 * w, stride=dil), :].astype(BF16)
        vs = vfull[pl.ds(kstart, 2 * w, stride=dil), :].astype(BF16)
        valid = band & ((kstart + dil * kj >= SUPER) | jnp.logical_not(first))
        outs, lses = [], []
        for hd in range(2):
            qm = jnp.where(head0 if hd == 0 else jnp.logical_not(head0), qs, 0.0).astype(BF16)
            s = _dot_nt(qm, ks)
            s = jnp.where(valid, s, -jnp.inf)
            m = jnp.max(s, axis=1, keepdims=True)
            p = jnp.exp(s - m)
            den = jnp.sum(p, axis=1, keepdims=True)
            outs.append(_dot(p.astype(BF16), vs) / den)
            lses.append(m + jnp.log(den))
        o_s[br, pl.ds(qstart, w, stride=dil), :] = jnp.where(head0, outs[0], outs[1])
        lse_s[br, pl.ds(qstart, w, stride=dil), :] = jnp.where(head0, lses[0], lses[1])

    for br, dil in enumerate(DILATIONS):
        span = w * dil

        def body(it, carry, br=br, dil=dil, span=span):
            c = it // dil
            r = it % dil
            sub_block(br, dil, c * span + r, SUPER + (c - 1) * span + r)
            return carry

        lax.fori_loop(0, SUPER // w, body, 0)

    l0, l1, l2 = lse_s[0], lse_s[1], lse_s[2]
    m = jnp.maximum(jnp.maximum(l0, l1), l2)
    e0, e1, e2 = jnp.exp(l0 - m), jnp.exp(l1 - m), jnp.exp(l2 - m)
    o_ref[...] = (e0 * o_s[0] + e1 * o_s[1] + e2 * o_s[2]) / (e0 + e1 + e2)


def _attention(proj, batch, seq):
    n = proj.shape[0]
    nsb = seq // SUPER
    hp = D_ATTN // LANES
    blk = (SUPER, LANES)
    cur = lambda off: (lambda b, h, s: (b * nsb + s, off + h))
    prev = lambda off: (lambda b, h, s: (b * nsb + jnp.maximum(s - 1, 0), off + h))
    return pl.pallas_call(
        _attn_kernel,
        grid=(batch, hp, nsb),
        in_specs=[
            pl.BlockSpec(blk, cur(0)),
            pl.BlockSpec(blk, prev(hp)),
            pl.BlockSpec(blk, cur(hp)),
            pl.BlockSpec(blk, prev(2 * hp)),
            pl.BlockSpec(blk, cur(2 * hp)),
        ],
        out_specs=pl.BlockSpec(blk, cur(0)),
        out_shape=jax.ShapeDtypeStruct((n, D_ATTN), F32),
        scratch_shapes=[
            pltpu.VMEM((2 * SUPER, LANES), F32),
            pltpu.VMEM((2 * SUPER, LANES), F32),
            pltpu.VMEM((len(DILATIONS), SUPER, LANES), F32),
            pltpu.VMEM((len(DILATIONS), SUPER, LANES), F32),
        ],
        compiler_params=_cparams("parallel", "parallel", "arbitrary"),
        name="dilated_attn",
    )(proj, proj, proj, proj, proj)


def _outproj_kernel(x_ref, ya_ref, u_ref, pw_ref, ps_ref, w_ref, g1_ref, o_ref, halo, ext,
                    *, tiles_per_seq):
    tm = x_ref.shape[0]
    ti = pl.program_id(0) % tiles_per_seq
    u = u_ref[...]

    @pl.when(ti == 0)
    def _():
        ext[0:POOL_HALO, :] = jnp.zeros((POOL_HALO, u.shape[1]), F32)

    @pl.when(ti != 0)
    def _():
        ext[0:POOL_HALO, :] = halo[...]

    ext[POOL_HALO:, :] = u
    halo[...] = u[tm - POOL_HALO:, :]

    pos1 = (ti * tm + 1 + lax.broadcasted_iota(jnp.int32, (tm, POOL_GROUP), 0)).astype(F32)
    ypool = []
    for gi, win in enumerate(POOL_WINDOWS):
        lo, hi = gi * POOL_GROUP, (gi + 1) * POOL_GROUP
        acc = ext[:, lo:hi]
        step = 1
        while step < win:
            acc = acc + pltpu.roll(acc, step, 0)
            step *= 2
        pooled = acc[POOL_HALO:, :] / jnp.minimum(pos1, float(win))
        diff = pooled - u[:, lo:hi]
        ypool.append(_dot(diff.astype(BF16), pw_ref[gi]) * ps_ref[:, lo:hi])
    ypool = jnp.concatenate(ypool, axis=1)
    dm = ya_ref.shape[1]
    y = _dot(ya_ref[...].astype(BF16), w_ref[0:dm, :]) + _dot(ypool.astype(BF16), w_ref[dm:, :])
    o_ref[...] = x_ref[...] + g1_ref[...] * y


def _outproj(x2, y_attn, proj, pool_w_bf16, pool_scale, w_bf16, g1, seq, tm):
    n, d = x2.shape
    dp = pool_scale.shape[1]
    tiles_per_seq = seq // tm
    bmap = lambda i: (i // tiles_per_seq, 0, 0)
    ucol = proj.shape[1] // dp - 1
    return pl.pallas_call(
        functools.partial(_outproj_kernel, tiles_per_seq=tiles_per_seq),
        grid=(n // tm,),
        in_specs=[
            pl.BlockSpec((tm, d), lambda i: (i, 0)),
            pl.BlockSpec((tm, D_ATTN), lambda i: (i, 0)),
            pl.BlockSpec((tm, dp), lambda i: (i, ucol)),
            pl.BlockSpec(pool_w_bf16.shape, lambda i: (0, 0, 0)),
            pl.BlockSpec((1, dp), lambda i: (0, 0)),
            pl.BlockSpec(w_bf16.shape, lambda i: (0, 0)),
            pl.BlockSpec((None, 1, d), bmap),
        ],
        out_specs=pl.BlockSpec((tm, d), lambda i: (i, 0)),
        out_shape=jax.ShapeDtypeStruct((n, d), F32),
        scratch_shapes=[
            pltpu.VMEM((POOL_HALO, dp), F32),
            pltpu.VMEM((tm + POOL_HALO, dp), F32),
        ],
        compiler_params=_cparams("arbitrary"),
        name="even_outproj",
    )(x2, y_attn, proj, pool_w_bf16, pool_scale, w_bf16, g1)


def _conv_kernel(x_ref, g_ref, sc_ref, sh_ref, win_ref, cw_ref, wout_ref, g1_ref, o_ref, carry, ext,
                 *, tiles_per_seq):
    tm, d = x_ref.shape
    ti = pl.program_id(0) % tiles_per_seq
    x = x_ref[...]
    h = _norm_mod(x, g_ref[...], sc_ref[...], sh_ref[...])
    p = _dot(h.astype(BF16), win_ref[...])
    u = p[:, 2 * d:] * p[:, 0:d]

    @pl.when(ti == 0)
    def _():
        ext[0:CONV_HALO, :] = jnp.zeros((CONV_HALO, d), F32)

    @pl.when(ti != 0)
    def _():
        ext[0:CONV_HALO, :] = carry[...]

    ext[CONV_HALO:, :] = u
    carry[...] = u[tm - CONV_HALO:, :]
    y = cw_ref[0:1, :] * u
    for lag in range(1, cw_ref.shape[0]):
        y = y + cw_ref[lag:lag + 1, :] * ext[pl.ds(CONV_HALO - lag, tm), :]
    z = p[:, d:2 * d] * y
    o_ref[...] = x + g1_ref[...] * _dot(z.astype(BF16), wout_ref[...])


def _conv_layer(x2, g, sc, sh, win_bf16, conv_w, wout_bf16, g1, seq, tm):
    n, d = x2.shape
    tiles_per_seq = seq // tm
    bmap = lambda i: (i // tiles_per_seq, 0, 0)
    return pl.pallas_call(
        functools.partial(_conv_kernel, tiles_per_seq=tiles_per_seq),
        grid=(n // tm,),
        in_specs=[
            pl.BlockSpec((tm, d), lambda i: (i, 0)),
            pl.BlockSpec((1, d), lambda i: (0, 0)),
            pl.BlockSpec((None, 1, d), bmap),
            pl.BlockSpec((None, 1, d), bmap),
            pl.BlockSpec(win_bf16.shape, lambda i: (0, 0)),
            pl.BlockSpec(conv_w.shape, lambda i: (0, 0)),
            pl.BlockSpec(wout_bf16.shape, lambda i: (0, 0)),
            pl.BlockSpec((None, 1, d), bmap),
        ],
        out_specs=pl.BlockSpec((tm, d), lambda i: (i, 0)),
        out_shape=jax.ShapeDtypeStruct((n, d), F32),
        scratch_shapes=[
            pltpu.VMEM((CONV_HALO, d), F32),
            pltpu.VMEM((tm + CONV_HALO, d), F32),
        ],
        compiler_params=_cparams("arbitrary"),
        name="conv_mixer",
    )(x2, g, sc, sh, win_bf16, conv_w, wout_bf16, g1)


def _topk_rows(s, k):
    rows, t = s.shape
    riota = lax.broadcasted_iota(jnp.int32, s.shape, 0)
    kiota = lax.broadcasted_iota(jnp.int32, (k, t), 0)
    vals = jnp.zeros((k, t), F32)
    idxs = jnp.zeros((k, t), jnp.int32)
    for j in range(k):
        m = jnp.max(s, axis=0, keepdims=True)
        i = jnp.min(jnp.where(s == m, riota, rows), axis=0, keepdims=True)
        vals = jnp.where(kiota == j, m, vals)
        idxs = jnp.where(kiota == j, i, idxs)
        s = jnp.where(riota == i, -jnp.inf, s)
    return vals, idxs


def _route_kernel(x_ref, g_ref, sc_ref, sh_ref, wh_ref, wl_ref, k1h_ref, k1l_ref, k2h_ref, k2l_ref,
                  h_ref, ids_ref, gate_ref):
    tr = x_ref.shape[0]
    h = _norm_mod(x_ref[...], g_ref[...], sc_ref[...], sh_ref[...])
    h_ref[...] = h
    hh, hl = _split_bf16(h)
    wh = wh_ref[...]
    q = _dot(hh, wh) + _dot(hl, wh) + _dot(hh, wl_ref[...])
    kiota = lax.broadcasted_iota(jnp.int32, (PEER_TOPK, tr), 0)

    def scores(kh_ref, kl_ref, qpart):
        qh, ql = _split_bf16(qpart)
        kh = kh_ref[...]
        return _dot_nt(kh, qh) + _dot_nt(kh, ql) + _dot_nt(kl_ref[...], qh)

    ids_all, gate_all = [], []
    for hd in range(PEER_HEADS):
        base = hd * 2 * PEER_HALF
        v1, i1 = _topk_rows(scores(k1h_ref, k1l_ref, q[:, base:base + PEER_HALF]), PEER_TOPK)
        v2, i2 = _topk_rows(scores(k2h_ref, k2l_ref, q[:, base + PEER_HALF:base + 2 * PEER_HALF]), PEER_TOPK)
        cand = jnp.concatenate([v1[i:i + 1, :] + v2 for i in range(PEER_TOPK)], axis=0)
        vs, ci = _topk_rows(cand, PEER_TOPK)
        ids = jnp.zeros((PEER_TOPK, tr), jnp.int32)
        for j in range(PEER_TOPK):
            c = ci[j:j + 1, :]
            e1 = jnp.sum(jnp.where(kiota == (c // PEER_TOPK), i1, 0), axis=0, keepdims=True)
            e2 = jnp.sum(jnp.where(kiota == (c % PEER_TOPK), i2, 0), axis=0, keepdims=True)
            ids = jnp.where(kiota == j, e1 * PEER_KEYS + e2, ids)
        p = jnp.exp(vs - vs[0:1, :])
        gate_all.append(p / jnp.sum(p, axis=0, keepdims=True))
        ids_all.append(ids)
    ids_ref[...] = jnp.concatenate(ids_all, axis=0).T
    gate_ref[...] = jnp.concatenate(gate_all, axis=0).T


def _route(x2, g, sc, sh, wq_hi, wq_lo, k1, k2, seq, tr):
    n, d = x2.shape
    tiles_per_seq = seq // tr
    bmap = lambda i: (i // tiles_per_seq, 0, 0)
    k1h, k1l = _split_bf16(k1)
    k2h, k2l = _split_bf16(k2)
    full2 = lambda a: pl.BlockSpec(a.shape, lambda i: (0, 0))
    return pl.pallas_call(
        _route_kernel,
        grid=(n // tr,),
        in_specs=[
            pl.BlockSpec((tr, d), lambda i: (i, 0)),
            pl.BlockSpec((1, d), lambda i: (0, 0)),
            pl.BlockSpec((None, 1, d), bmap),
            pl.BlockSpec((None, 1, d), bmap),
            full2(wq_hi), full2(wq_lo), full2(k1h), full2(k1l), full2(k2h), full2(k2l),
        ],
        out_specs=[
            pl.BlockSpec((tr, d), lambda i: (i, 0)),
            pl.BlockSpec((tr, PEER_SLOTS), lambda i: (i, 0)),
            pl.BlockSpec((tr, PEER_SLOTS), lambda i: (i, 0)),
        ],
        out_shape=[
            jax.ShapeDtypeStruct((n, d), F32),
            jax.ShapeDtypeStruct((n, PEER_SLOTS), jnp.int32),
            jax.ShapeDtypeStruct((n, PEER_SLOTS), F32),
        ],
        compiler_params=_cparams("parallel"),
        name="peer_route",
    )(x2, g, sc, sh, wq_hi, wq_lo, k1h, k1l, k2h, k2l)


PEER_BUFS = 4


def _gather_kernel(ids_hbm, gate_ref, h_ref, x_ref, g2_ref, uv_hbm, o_ref, ids_smem, buf, sems, ids_sem):
    tg, d = x_ref.shape
    step = pl.program_id(0)
    ids_copy = pltpu.make_async_copy(
        ids_hbm.at[pl.ds(step * tg * PEER_SLOTS, tg * PEER_SLOTS)], ids_smem, ids_sem)
    ids_copy.start()
    ids_copy.wait()

    def row_copy(e, slot, a):
        return pltpu.make_async_copy(
            uv_hbm.at[pl.ds(e, 1), :], buf.at[slot, pl.ds(a, 1), :], sems.at[slot])

    def issue(t, slot):
        for a in range(PEER_SLOTS):
            row_copy(ids_smem[t * PEER_SLOTS + a], slot, a).start()

    def wait(slot):
        pltpu.make_async_copy(uv_hbm.at[pl.ds(0, PEER_SLOTS), :], buf.at[slot], sems.at[slot]).wait()

    eye = (lax.broadcasted_iota(jnp.int32, (PEER_SLOTS, PEER_SLOTS), 0)
           == lax.broadcasted_iota(jnp.int32, (PEER_SLOTS, PEER_SLOTS), 1))
    g2 = g2_ref[...]

    def compute(t, slot):
        hrow = h_ref[pl.ds(t, 1), :]
        act = jnp.sum(buf[slot, :, 0:d] * hrow, axis=1, keepdims=True)
        gelu = 0.5 * act * (1.0 + lax.erf(act * (1.0 / math.sqrt(2.0))))
        gcol = jnp.sum(jnp.where(eye, gate_ref[pl.ds(t, 1), :], 0.0), axis=1, keepdims=True)
        y = jnp.sum(buf[slot, :, d:2 * d] * (gcol * gelu), axis=0, keepdims=True)
        o_ref[pl.ds(t, 1), :] = x_ref[pl.ds(t, 1), :] + g2 * y

    for s in range(PEER_BUFS - 1):
        issue(s, s)

    def group(j, carry):
        for s in range(PEER_BUFS):
            t = j * PEER_BUFS + s
            ahead = t + PEER_BUFS - 1

            @pl.when(ahead < tg)
            def _():
                issue(ahead, (s + PEER_BUFS - 1) % PEER_BUFS)

            wait(s)
            compute(t, s)
        return carry

    lax.fori_loop(0, tg // PEER_BUFS, group, 0)


def _gather_mix(ids, gates, h2, x2, g2, uv, seq, tg):
    n, d = x2.shape
    tiles_per_seq = seq // tg
    bmap = lambda i: (i // tiles_per_seq, 0, 0)
    return pl.pallas_call(
        _gather_kernel,
        grid=(n // tg,),
        in_specs=[
            pl.BlockSpec(memory_space=pl.ANY),
            pl.BlockSpec((tg, PEER_SLOTS), lambda i: (i, 0)),
            pl.BlockSpec((tg, d), lambda i: (i, 0)),
            pl.BlockSpec((tg, d), lambda i: (i, 0)),
            pl.BlockSpec((None, 1, d), bmap),
            pl.BlockSpec(memory_space=pl.ANY),
        ],
        out_specs=pl.BlockSpec((tg, d), lambda i: (i, 0)),
        out_shape=jax.ShapeDtypeStruct((n, d), F32),
        scratch_shapes=[
            pltpu.SMEM((tg * PEER_SLOTS,), jnp.int32),
            pltpu.VMEM((PEER_BUFS, PEER_SLOTS, 2 * d), F32),
            pltpu.SemaphoreType.DMA((PEER_BUFS,)),
            pltpu.SemaphoreType.DMA,
        ],
        compiler_params=_cparams("arbitrary"),
        name="peer_gather",
    )(ids.reshape(-1), gates, h2, x2, g2, uv)


def _pick_tile(seq, want):
    t = min(want, seq)
    while seq % t:
        t //= 2
    return t


def kernel(x, c, ada_w, ada_b, norm_mix_g, norm_ffn_g, ab_w_in, ab_q_gain, ab_k_gain, pool_w, pool_scale, ab_w_out, conv_w_in, conv_w, conv_w_out, peer_w_q, peer_key1, peer_key2, peer_u, peer_v):
    b, s, d = x.shape
    depth = ada_w.shape[0]
    assert s % SUPER == 0 and d % LANES == 0
    x2 = x.reshape(b * s, d)
    mods = _ada_mod(c, ada_w, ada_b).reshape(depth, b, 6, 1, d)

    tm_dense = _pick_tile(s, 512)
    tm_conv = _pick_tile(s, 256)
    tm_peer = _pick_tile(s, 256)

    for layer in range(depth):
        sh1, sc1, g1, sh2, sc2, g2 = [mods[layer, :, i] for i in range(6)]
        gm = norm_mix_g[layer].reshape(1, d)
        if layer % 2 == 0:
            e = layer // 2
            proj = _inproj(x2, gm, sc1, sh1, ab_w_in[e].astype(BF16),
                           ab_q_gain[e].reshape(1, D_ATTN), ab_k_gain[e].reshape(1, D_ATTN), s, tm_dense)
            y_attn = _attention(proj, b, s)
            x2 = _outproj(x2, y_attn, proj, pool_w[e].astype(BF16), pool_scale[e].reshape(1, -1),
                          ab_w_out[e].astype(BF16), g1, s, tm_dense)
        else:
            o = layer // 2
            x2 = _conv_layer(x2, gm, sc1, sh1, conv_w_in[o].astype(BF16), conv_w[o],
                             conv_w_out[o].astype(BF16), g1, s, tm_conv)
        wq_hi, wq_lo = _split_bf16(peer_w_q[layer])
        h2, ids, gates = _route(x2, norm_ffn_g[layer].reshape(1, d), sc2, sh2, wq_hi, wq_lo,
                                peer_key1[layer], peer_key2[layer], s, tm_peer)
        uv = jnp.concatenate([peer_u[layer], peer_v[layer]], axis=1)
        x2 = _gather_mix(ids, gates, h2, x2, g2, uv, s, tm_peer)
    return x2.reshape(b, s, d)
```

```python
import functools
import math

import jax
import jax.numpy as jnp
from jax import lax
from jax.experimental import pallas as pl
from jax.experimental.pallas import tpu as pltpu

F32 = jnp.float32
BF16 = jnp.bfloat16
EPS = 1e-6

N_HEADS = 8
HEAD_DIM = 64
D_ATTN = N_HEADS * HEAD_DIM
ATTN_WINDOW_STEPS = 128
DILATIONS = (1, 4, 16)
SUPER = ATTN_WINDOW_STEPS * DILATIONS[-1]
POOL_WINDOWS = (2, 4, 8, 16)
POOL_GROUP = 128
POOL_HALO = 16
CONV_HALO = 8
PEER_HEADS = 8
PEER_KEYS = 128
PEER_TOPK = 16
PEER_HALF = 128
PEER_SLOTS = PEER_HEADS * PEER_TOPK
LANES = 128
VMEM_LIMIT = 56 * 1024 * 1024


def _cparams(*sem):
    return pltpu.CompilerParams(dimension_semantics=sem, vmem_limit_bytes=VMEM_LIMIT)


def _split_bf16(x):
    hi = x.astype(BF16)
    lo = (x - hi.astype(F32)).astype(BF16)
    return hi, lo


def _dot(a, b):
    return jnp.dot(a, b, preferred_element_type=F32)


def _dot_nt(a, b):
    return lax.dot_general(a, b, (((1,), (1,)), ((), ())), preferred_element_type=F32)


def _norm_mod(x, g, sc, sh):
    ms = jnp.mean(x * x, axis=-1, keepdims=True)
    y = x * lax.rsqrt(ms + EPS)
    return (y * g) * (1.0 + sc) + sh


def _ada_kernel(c_ref, w_ref, b_ref, o_ref):
    c = c_ref[...]
    a = c * jax.nn.sigmoid(c)
    ah, al = _split_bf16(a)
    wh, wl = _split_bf16(w_ref[...])
    o_ref[...] = _dot(ah, wh) + _dot(al, wh) + _dot(ah, wl) + b_ref[...]


def _ada_mod(c, ada_w, ada_b):
    depth, d, d6 = ada_w.shape
    b = c.shape[0]
    rows = 8
    c_pad = jnp.zeros((rows, d), F32).at[:b].set(c)
    tn = 768
    out = pl.pallas_call(
        _ada_kernel,
        grid=(depth, d6 // tn),
        in_specs=[
            pl.BlockSpec((rows, d), lambda l, j: (0, 0)),
            pl.BlockSpec((None, d, tn), lambda l, j: (l, 0, j)),
            pl.BlockSpec((None, 1, tn), lambda l, j: (l, 0, j)),
        ],
        out_specs=pl.BlockSpec((None, rows, tn), lambda l, j: (l, 0, j)),
        out_shape=jax.ShapeDtypeStruct((depth, rows, d6), F32),
        compiler_params=_cparams("parallel", "parallel"),
        name="ada_mod",
    )(c_pad, ada_w, ada_b.reshape(depth, 1, d6))
    return out[:, :b, :]


def _inproj_kernel(x_ref, g_ref, sc_ref, sh_ref, w_ref, qg_ref, kg_ref, hm_ref, o_ref):
    h = _norm_mod(x_ref[...], g_ref[...], sc_ref[...], sh_ref[...])
    proj = _dot(h.astype(BF16), w_ref[...])
    hm = hm_ref[...]

    def head_norm(z, gain, scale):
        sh_, sl_ = _split_bf16(z * z)
        ss = _dot(sh_, hm) + _dot(sl_, hm)
        return z * lax.rsqrt(ss * (1.0 / HEAD_DIM) + EPS) * (gain * scale)

    o_ref[:, 0:D_ATTN] = head_norm(proj[:, 0:D_ATTN], qg_ref[...], HEAD_DIM ** -0.5)
    o_ref[:, D_ATTN:2 * D_ATTN] = head_norm(proj[:, D_ATTN:2 * D_ATTN], kg_ref[...], 1.0)
    o_ref[:, 2 * D_ATTN:] = proj[:, 2 * D_ATTN:]


def _inproj(x2, g, sc, sh, w_bf16, q_gain, k_gain, seq, tm):
    n, d = x2.shape
    dout = w_bf16.shape[1]
    tiles_per_seq = seq // tm
    head_of_lane = jnp.arange(D_ATTN) // HEAD_DIM
    hm = (head_of_lane[:, None] == head_of_lane[None, :]).astype(BF16)
    bmap = lambda i: (i // tiles_per_seq, 0, 0)
    return pl.pallas_call(
        _inproj_kernel,
        grid=(n // tm,),
        in_specs=[
            pl.BlockSpec((tm, d), lambda i: (i, 0)),
            pl.BlockSpec((1, d), lambda i: (0, 0)),
            pl.BlockSpec((None, 1, d), bmap),
            pl.BlockSpec((None, 1, d), bmap),
            pl.BlockSpec((d, dout), lambda i: (0, 0)),
            pl.BlockSpec((1, D_ATTN), lambda i: (0, 0)),
            pl.BlockSpec((1, D_ATTN), lambda i: (0, 0)),
            pl.BlockSpec((D_ATTN, D_ATTN), lambda i: (0, 0)),
        ],
        out_specs=pl.BlockSpec((tm, dout), lambda i: (i, 0)),
        out_shape=jax.ShapeDtypeStruct((n, dout), F32),
        compiler_params=_cparams("parallel"),
        name="even_inproj",
    )(x2, g, sc, sh, w_bf16, q_gain, k_gain, hm)


def _attn_kernel(q_ref, kp_ref, kc_ref, vp_ref, vc_ref, o_ref, kfull, vfull, o_s, lse_s):
    w = ATTN_WINDOW_STEPS
    first = pl.program_id(2) == 0
    kfull[0:SUPER, :] = kp_ref[...]
    kfull[SUPER:2 * SUPER, :] = kc_ref[...]
    vfull[0:SUPER, :] = vp_ref[...]
    vfull[SUPER:2 * SUPER, :] = vc_ref[...]

    qi = lax.broadcasted_iota(jnp.int32, (w, 2 * w), 0)
    kj = lax.broadcasted_iota(jnp.int32, (w, 2 * w), 1)
    delta = qi + w - kj
    band = (delta >= 0) & (delta <= w)
    lane = lax.broadcasted_iota(jnp.int32, (w, LANES), 1)
    head0 = lane < HEAD_DIM

    def sub_block(br, dil, qstart, kstart):
        qs = q_ref[pl.ds(qstart, w, stride=dil), :]
        ks = kfull[pl.ds(kstart, 2 * w, stride=dil), :].astype(BF16)
        vs = vfull[pl.ds(kstart, 2 * w, stride=dil), :].astype(BF16)
        valid = band & ((kstart + dil * kj >= SUPER) | jnp.logical_not(first))
        outs, lses = [], []
        for hd in range(2):
            qm = jnp.where(head0 if hd == 0 else jnp.logical_not(head0), qs, 0.0).astype(BF16)
            s = _dot_nt(qm, ks)
            s = jnp.where(valid, s, -jnp.inf)
            m = jnp.max(s, axis=1, keepdims=True)
            p = jnp.exp(s - m)
            den = jnp.sum(p, axis=1, keepdims=True)
            outs.append(_dot(p.astype(BF16), vs) / den)
            lses.append(m + jnp.log(den))
        o_s[br, pl.ds(qstart, w, stride=dil), :] = jnp.where(head0, outs[0], outs[1])
        lse_s[br, pl.ds(qstart, w, stride=dil), :] = jnp.where(head0, lses[0], lses[1])

    for br, dil in enumerate(DILATIONS):
        span = w * dil

        def body(it, carry, br=br, dil=dil, span=span):
            c = it // dil
            r = it % dil
            sub_block(br, dil, c * span + r, SUPER + (c - 1) * span + r)
            return carry

        lax.fori_loop(0, SUPER // w, body, 0)

    l0, l1, l2 = lse_s[0], lse_s[1], lse_s[2]
    m = jnp.maximum(jnp.maximum(l0, l1), l2)
    e0, e1, e2 = jnp.exp(l0 - m), jnp.exp(l1 - m), jnp.exp(l2 - m)
    o_ref[...] = (e0 * o_s[0] + e1 * o_s[1] + e2 * o_s[2]) / (e0 + e1 + e2)


def _attention(proj, batch, seq):
    n = proj.shape[0]
    nsb = seq // SUPER
    hp = D_ATTN // LANES
    blk = (SUPER, LANES)
    cur = lambda off: (lambda b, h, s: (b * nsb + s, off + h))
    prev = lambda off: (lambda b, h, s: (b * nsb + jnp.maximum(s - 1, 0), off + h))
    return pl.pallas_call(
        _attn_kernel,
        grid=(batch, hp, nsb),
        in_specs=[
            pl.BlockSpec(blk, cur(0)),
            pl.BlockSpec(blk, prev(hp)),
            pl.BlockSpec(blk, cur(hp)),
            pl.BlockSpec(blk, prev(2 * hp)),
            pl.BlockSpec(blk, cur(2 * hp)),
        ],
        out_specs=pl.BlockSpec(blk, cur(0)),
        out_shape=jax.ShapeDtypeStruct((n, D_ATTN), F32),
        scratch_shapes=[
            pltpu.VMEM((2 * SUPER, LANES), F32),
            pltpu.VMEM((2 * SUPER, LANES), F32),
            pltpu.VMEM((len(DILATIONS), SUPER, LANES), F32),
            pltpu.VMEM((len(DILATIONS), SUPER, LANES), F32),
        ],
        compiler_params=_cparams("parallel", "parallel", "arbitrary"),
        name="dilated_attn",
    )(proj, proj, proj, proj, proj)


def _outproj_kernel(x_ref, ya_ref, u_ref, pw_ref, ps_ref, w_ref, g1_ref, o_ref, halo, ext,
                    *, tiles_per_seq):
    tm = x_ref.shape[0]
    ti = pl.program_id(0) % tiles_per_seq
    u = u_ref[...]

    @pl.when(ti == 0)
    def _():
        ext[0:POOL_HALO, :] = jnp.zeros((POOL_HALO, u.shape[1]), F32)

    @pl.when(ti != 0)
    def _():
        ext[0:POOL_HALO, :] = halo[...]

    ext[POOL_HALO:, :] = u
    halo[...] = u[tm - POOL_HALO:, :]

    pos1 = (ti * tm + 1 + lax.broadcasted_iota(jnp.int32, (tm, POOL_GROUP), 0)).astype(F32)
    ypool = []
    for gi, win in enumerate(POOL_WINDOWS):
        lo, hi = gi * POOL_GROUP, (gi + 1) * POOL_GROUP
        acc = ext[:, lo:hi]
        step = 1
        while step < win:
            acc = acc + pltpu.roll(acc, step, 0)
            step *= 2
        pooled = acc[POOL_HALO:, :] / jnp.minimum(pos1, float(win))
        diff = pooled - u[:, lo:hi]
        ypool.append(_dot(diff.astype(BF16), pw_ref[gi]) * ps_ref[:, lo:hi])
    ypool = jnp.concatenate(ypool, axis=1)
    dm = ya_ref.shape[1]
    y = _dot(ya_ref[...].astype(BF16), w_ref[0:dm, :]) + _dot(ypool.astype(BF16), w_ref[dm:, :])
    o_ref[...] = x_ref[...] + g1_ref[...] * y


def _outproj(x2, y_attn, proj, pool_w_bf16, pool_scale, w_bf16, g1, seq, tm):
    n, d = x2.shape
    dp = pool_scale.shape[1]
    tiles_per_seq = seq // tm
    bmap = lambda i: (i // tiles_per_seq, 0, 0)
    ucol = proj.shape[1] // dp - 1
    return pl.pallas_call(
        functools.partial(_outproj_kernel, tiles_per_seq=tiles_per_seq),
        grid=(n // tm,),
        in_specs=[
            pl.BlockSpec((tm, d), lambda i: (i, 0)),
            pl.BlockSpec((tm, D_ATTN), lambda i: (i, 0)),
            pl.BlockSpec((tm, dp), lambda i: (i, ucol)),
            pl.BlockSpec(pool_w_bf16.shape, lambda i: (0, 0, 0)),
            pl.BlockSpec((1, dp), lambda i: (0, 0)),
            pl.BlockSpec(w_bf16.shape, lambda i: (0, 0)),
            pl.BlockSpec((None, 1, d), bmap),
        ],
        out_specs=pl.BlockSpec((tm, d), lambda i: (i, 0)),
        out_shape=jax.ShapeDtypeStruct((n, d), F32),
        scratch_shapes=[
            pltpu.VMEM((POOL_HALO, dp), F32),
            pltpu.VMEM((tm + POOL_HALO, dp), F32),
        ],
        compiler_params=_cparams("arbitrary"),
        name="even_outproj",
    )(x2, y_attn, proj, pool_w_bf16, pool_scale, w_bf16, g1)


def _conv_kernel(x_ref, g_ref, sc_ref, sh_ref, win_ref, cw_ref, wout_ref, g1_ref, o_ref, carry, ext,
                 *, tiles_per_seq):
    tm, d = x_ref.shape
    ti = pl.program_id(0) % tiles_per_seq
    x = x_ref[...]
    h = _norm_mod(x, g_ref[...], sc_ref[...], sh_ref[...])
    p = _dot(h.astype(BF16), win_ref[...])
    u = p[:, 2 * d:] * p[:, 0:d]

    @pl.when(ti == 0)
    def _():
        ext[0:CONV_HALO, :] = jnp.zeros((CONV_HALO, d), F32)

    @pl.when(ti != 0)
    def _():
        ext[0:CONV_HALO, :] = carry[...]

    ext[CONV_HALO:, :] = u
    carry[...] = u[tm - CONV_HALO:, :]
    y = cw_ref[0:1, :] * u
    for lag in range(1, cw_ref.shape[0]):
        y = y + cw_ref[lag:lag + 1, :] * ext[pl.ds(CONV_HALO - lag, tm), :]
    z = p[:, d:2 * d] * y
    o_ref[...] = x + g1_ref[...] * _dot(z.astype(BF16), wout_ref[...])


def _conv_layer(x2, g, sc, sh, win_bf16, conv_w, wout_bf16, g1, seq, tm):
    n, d = x2.shape
    tiles_per_seq = seq // tm
    bmap = lambda i: (i // tiles_per_seq, 0, 0)
    return pl.pallas_call(
        functools.partial(_conv_kernel, tiles_per_seq=tiles_per_seq),
        grid=(n // tm,),
        in_specs=[
            pl.BlockSpec((tm, d), lambda i: (i, 0)),
            pl.BlockSpec((1, d), lambda i: (0, 0)),
            pl.BlockSpec((None, 1, d), bmap),
            pl.BlockSpec((None, 1, d), bmap),
            pl.BlockSpec(win_bf16.shape, lambda i: (0, 0)),
            pl.BlockSpec(conv_w.shape, lambda i: (0, 0)),
            pl.BlockSpec(wout_bf16.shape, lambda i: (0, 0)),
            pl.BlockSpec((None, 1, d), bmap),
        ],
        out_specs=pl.BlockSpec((tm, d), lambda i: (i, 0)),
        out_shape=jax.ShapeDtypeStruct((n, d), F32),
        scratch_shapes=[
            pltpu.VMEM((CONV_HALO, d), F32),
            pltpu.VMEM((tm + CONV_HALO, d), F32),
        ],
        compiler_params=_cparams("arbitrary"),
        name="conv_mixer",
    )(x2, g, sc, sh, win_bf16, conv_w, wout_bf16, g1)


PEER_CAND_ROWS = 80


def _topk_rows(ss, key, k):
    t = ss[0].shape[1]
    kiota = lax.broadcasted_iota(jnp.int32, (k, t), 0)
    ss = list(ss)
    vals = [jnp.zeros((k, t), F32) for _ in ss]
    keys = [jnp.zeros((k, t), F32) for _ in ss]
    for j in range(k):
        for n, s in enumerate(ss):
            m = jnp.max(s, axis=0, keepdims=True)
            i = jnp.min(jnp.where(s == m, key, 1e9), axis=0, keepdims=True)
            vals[n] = jnp.where(kiota == j, m, vals[n])
            keys[n] = jnp.where(kiota == j, i, keys[n])
            ss[n] = jnp.where(key == i, -jnp.inf, s)
    return list(zip(vals, keys))


def _route_kernel(x_ref, g_ref, sc_ref, sh_ref, wh_ref, wl_ref, k1h_ref, k1l_ref, k2h_ref, k2l_ref,
                  h_ref, ids_ref, gate_ref, q_s, ids_s, gate_s):
    tr = x_ref.shape[0]
    nhalf = tr // LANES
    k = PEER_TOPK
    h = _norm_mod(x_ref[...], g_ref[...], sc_ref[...], sh_ref[...])
    h_ref[...] = h
    hh, hl = _split_bf16(h)
    wh = wh_ref[...]
    q = _dot(hh, wh) + _dot(hl, wh) + _dot(hh, wl_ref[...])
    for p in range(2 * PEER_HEADS):
        q_s[p] = q[:, p * PEER_HALF:(p + 1) * PEER_HALF]

    def scores(kh_ref, kl_ref, qpart):
        qh, ql = _split_bf16(qpart)
        kh = kh_ref[...]
        return _dot_nt(kh, qh) + _dot_nt(kh, ql) + _dot_nt(kl_ref[...], qh)

    def body(it, carry):
        hd = it // nhalf
        half = it % nhalf
        r0 = pl.multiple_of(half * LANES, LANES)
        key_rows = lax.broadcasted_iota(jnp.int32, (PEER_KEYS, LANES), 0).astype(F32)
        (v1, i1), (v2, i2) = _topk_rows(
            [scores(k1h_ref, k1l_ref, q_s[2 * hd, pl.ds(r0, LANES), :]),
             scores(k2h_ref, k2l_ref, q_s[2 * hd + 1, pl.ds(r0, LANES), :])], key_rows, k)
        i1 = i1.astype(jnp.int32)
        i2 = i2.astype(jnp.int32)

        r = lax.broadcasted_iota(jnp.int32, (PEER_CAND_ROWS, LANES), 0)
        blk, j8 = r >> 3, r & 7
        flat = jnp.where(blk <= 1, r, jnp.where(blk <= 8, k * (blk - 1) + j8, k * (8 + j8)))
        jlim = jnp.where(blk == 3, 5, jnp.where(blk == 4, 4, jnp.where(blk == 5, 3, 2)))
        valid = (blk <= 2) | (blk == 9) | (j8 < jlim)
        cand = jnp.concatenate(
            [v1[0:1, :] + v2]
            + [v1[i:i + 1, :] + v2[0:8, :] for i in range(1, 8)]
            + [v1[8:16, :] + v2[0:1, :]], axis=0)
        cand = jnp.where(valid, cand, -jnp.inf)
        ((vs, cf),) = _topk_rows([cand], flat.astype(F32), k)
        cf = cf.astype(jnp.int32)

        kiota = lax.broadcasted_iota(jnp.int32, (k, LANES), 0)
        ids = jnp.zeros((k, LANES), jnp.int32)
        for j in range(k):
            c = cf[j:j + 1, :]
            e1 = jnp.sum(jnp.where(kiota == (c >> 4), i1, 0), axis=0, keepdims=True)
            e2 = jnp.sum(jnp.where(kiota == (c & (k - 1)), i2, 0), axis=0, keepdims=True)
            ids = jnp.where(kiota == j, e1 * PEER_KEYS + e2, ids)
        p = jnp.exp(vs - vs[0:1, :])
        row0 = pl.multiple_of(hd * k, k)
        ids_s[half, pl.ds(row0, k), :] = ids
        gate_s[half, pl.ds(row0, k), :] = p / jnp.sum(p, axis=0, keepdims=True)
        return carry

    lax.fori_loop(0, PEER_HEADS * nhalf, body, 0)
    for half in range(nhalf):
        ids_ref[half * LANES:(half + 1) * LANES, :] = ids_s[half].T
        gate_ref[half * LANES:(half + 1) * LANES, :] = gate_s[half].T


def _route(x2, g, sc, sh, wq_hi, wq_lo, k1, k2, seq, tr):
    n, d = x2.shape
    assert PEER_TOPK == 16 and tr % LANES == 0
    tiles_per_seq = seq // tr
    bmap = lambda i: (i // tiles_per_seq, 0, 0)
    k1h, k1l = _split_bf16(k1)
    k2h, k2l = _split_bf16(k2)
    full2 = lambda a: pl.BlockSpec(a.shape, lambda i: (0, 0))
    return pl.pallas_call(
        _route_kernel,
        grid=(n // tr,),
        in_specs=[
            pl.BlockSpec((tr, d), lambda i: (i, 0)),
            pl.BlockSpec((1, d), lambda i: (0, 0)),
            pl.BlockSpec((None, 1, d), bmap),
            pl.BlockSpec((None, 1, d), bmap),
            full2(wq_hi), full2(wq_lo), full2(k1h), full2(k1l), full2(k2h), full2(k2l),
        ],
        out_specs=[
            pl.BlockSpec((tr, d), lambda i: (i, 0)),
            pl.BlockSpec((tr, PEER_SLOTS), lambda i: (i, 0)),
            pl.BlockSpec((tr, PEER_SLOTS), lambda i: (i, 0)),
        ],
        out_shape=[
            jax.ShapeDtypeStruct((n, d), F32),
            jax.ShapeDtypeStruct((n, PEER_SLOTS), jnp.int32),
            jax.ShapeDtypeStruct((n, PEER_SLOTS), F32),
        ],
        scratch_shapes=[
            pltpu.VMEM((2 * PEER_HEADS, tr, PEER_HALF), F32),
            pltpu.VMEM((tr // LANES, PEER_SLOTS, LANES), jnp.int32),
            pltpu.VMEM((tr // LANES, PEER_SLOTS, LANES), F32),
        ],
        compiler_params=_cparams("parallel"),
        name="peer_route",
    )(x2, g, sc, sh, wq_hi, wq_lo, k1h, k1l, k2h, k2l)


PEER_BUFS = 8


def _gather_kernel(ids_hbm, gate_ref, h_ref, x_ref, g2_ref, uv_hbm, o_ref, ids_smem, *scratch):
    bufs, sems, ids_sem = scratch[:PEER_BUFS], scratch[PEER_BUFS], scratch[PEER_BUFS + 1]
    tg, d = x_ref.shape
    step = pl.program_id(0)
    ids_copy = pltpu.make_async_copy(
        ids_hbm.at[pl.ds(step * tg * PEER_SLOTS, tg * PEER_SLOTS)], ids_smem, ids_sem)
    ids_copy.start()
    ids_copy.wait()

    def issue(t, slot):
        for a in range(PEER_SLOTS):
            e = ids_smem[t * PEER_SLOTS + a]
            pltpu.make_async_copy(uv_hbm.at[pl.ds(e, 1), :], bufs[slot].at[pl.ds(a, 1), :], sems.at[slot]).start()

    def wait(slot):
        pltpu.make_async_copy(uv_hbm.at[pl.ds(0, PEER_SLOTS), :], bufs[slot], sems.at[slot]).wait()

    eye = (lax.broadcasted_iota(jnp.int32, (PEER_SLOTS, PEER_SLOTS), 0)
           == lax.broadcasted_iota(jnp.int32, (PEER_SLOTS, PEER_SLOTS), 1))
    g2 = g2_ref[...]

    def compute(t, slot):
        buf = bufs[slot]
        hrow = h_ref[pl.ds(t, 1), :]
        act = jnp.sum(buf[:, 0:d] * hrow, axis=1, keepdims=True)
        gelu = 0.5 * act * (1.0 + lax.erf(act * (1.0 / math.sqrt(2.0))))
        gcol = jnp.sum(jnp.where(eye, gate_ref[pl.ds(t, 1), :], 0.0), axis=1, keepdims=True)
        y = jnp.sum(buf[:, d:2 * d] * (gcol * gelu), axis=0, keepdims=True)
        o_ref[pl.ds(t, 1), :] = x_ref[pl.ds(t, 1), :] + g2 * y

    ahead = PEER_BUFS - 1
    for s in range(ahead):
        issue(s, s)

    def token(t, s, refill):
        wait(s)
        if refill:
            issue(t + ahead, (s + ahead) % PEER_BUFS)
        compute(t, s)

    def group(j, carry):
        for s in range(PEER_BUFS):
            token(j * PEER_BUFS + s, s, True)
        return carry

    n_groups = tg // PEER_BUFS
    lax.fori_loop(0, n_groups - 1, group, 0)
    for s in range(PEER_BUFS):
        t = (n_groups - 1) * PEER_BUFS + s
        token(t, s, t + ahead < tg)


def _gather_mix(ids, gates, h2, x2, g2, uv, seq, tg):
    n, d = x2.shape
    assert tg % PEER_BUFS == 0 and tg >= 2 * PEER_BUFS
    tiles_per_seq = seq // tg
    bmap = lambda i: (i // tiles_per_seq, 0, 0)
    return pl.pallas_call(
        _gather_kernel,
        grid=(n // tg,),
        in_specs=[
            pl.BlockSpec(memory_space=pl.ANY),
            pl.BlockSpec((tg, PEER_SLOTS), lambda i: (i, 0)),
            pl.BlockSpec((tg, d), lambda i: (i, 0)),
            pl.BlockSpec((tg, d), lambda i: (i, 0)),
            pl.BlockSpec((None, 1, d), bmap),
            pl.BlockSpec(memory_space=pl.ANY),
        ],
        out_specs=pl.BlockSpec((tg, d), lambda i: (i, 0)),
        out_shape=jax.ShapeDtypeStruct((n, d), F32),
        scratch_shapes=(
            [pltpu.SMEM((tg * PEER_SLOTS,), jnp.int32)]
            + [pltpu.VMEM((PEER_SLOTS, 2 * d), F32) for _ in range(PEER_BUFS)]
            + [pltpu.SemaphoreType.DMA((PEER_BUFS,)), pltpu.SemaphoreType.DMA]
        ),
        compiler_params=_cparams("arbitrary"),
        name="peer_gather",
    )(ids.reshape(-1), gates, h2, x2, g2, uv)


def _pick_tile(seq, want):
    t = min(want, seq)
    while seq % t:
        t //= 2
    return t


def kernel(x, c, ada_w, ada_b, norm_mix_g, norm_ffn_g, ab_w_in, ab_q_gain, ab_k_gain, pool_w, pool_scale, ab_w_out, conv_w_in, conv_w, conv_w_out, peer_w_q, peer_key1, peer_key2, peer_u, peer_v):
    b, s, d = x.shape
    depth = ada_w.shape[0]
    assert s % SUPER == 0 and d % LANES == 0
    x2 = x.reshape(b * s, d)
    mods = _ada_mod(c, ada_w, ada_b).reshape(depth, b, 6, 1, d)

    tm_dense = _pick_tile(s, 512)
    tm_conv = _pick_tile(s, 256)
    tm_peer = _pick_tile(s, 256)

    for layer in range(depth):
        sh1, sc1, g1, sh2, sc2, g2 = [mods[layer, :, i] for i in range(6)]
        gm = norm_mix_g[layer].reshape(1, d)
        if layer % 2 == 0:
            e = layer // 2
            proj = _inproj(x2, gm, sc1, sh1, ab_w_in[e].astype(BF16),
                           ab_q_gain[e].reshape(1, D_ATTN), ab_k_gain[e].reshape(1, D_ATTN), s, tm_dense)
            y_attn = _attention(proj, b, s)
            x2 = _outproj(x2, y_attn, proj, pool_w[e].astype(BF16), pool_scale[e].reshape(1, -1),
                          ab_w_out[e].astype(BF16), g1, s, tm_dense)
        else:
            o = layer // 2
            x2 = _conv_layer(x2, gm, sc1, sh1, conv_w_in[o].astype(BF16), conv_w[o],
                             conv_w_out[o].astype(BF16), g1, s, tm_conv)
        wq_hi, wq_lo = _split_bf16(peer_w_q[layer])
        h2, ids, gates = _route(x2, norm_ffn_g[layer].reshape(1, d), sc2, sh2, wq_hi, wq_lo,
                                peer_key1[layer], peer_key2[layer], s, tm_peer)
        uv = jnp.concatenate([peer_u[layer], peer_v[layer]], axis=1)
        x2 = _gather_mix(ids, gates, h2, x2, g2, uv, s, tm_peer)
    return x2.reshape(b, s, d)
```

```python
import functools
import math

import jax
import jax.numpy as jnp
from jax import lax
from jax.experimental import pallas as pl
from jax.experimental.pallas import tpu as pltpu

F32 = jnp.float32
BF16 = jnp.bfloat16
EPS = 1e-6

N_HEADS = 8
HEAD_DIM = 64
D_ATTN = N_HEADS * HEAD_DIM
ATTN_WINDOW_STEPS = 128
DILATIONS = (1, 4, 16)
SUPER = ATTN_WINDOW_STEPS * DILATIONS[-1]
POOL_WINDOWS = (2, 4, 8, 16)
POOL_GROUP = 128
POOL_HALO = 16
CONV_HALO = 8
PEER_HEADS = 8
PEER_KEYS = 128
PEER_TOPK = 16
PEER_HALF = 128
PEER_SLOTS = PEER_HEADS * PEER_TOPK
LANES = 128
VMEM_LIMIT = 56 * 1024 * 1024


def _cparams(*sem):
    return pltpu.CompilerParams(dimension_semantics=sem, vmem_limit_bytes=VMEM_LIMIT)


def _split_bf16(x):
    hi = x.astype(BF16)
    lo = (x - hi.astype(F32)).astype(BF16)
    return hi, lo


def _dot(a, b):
    return jnp.dot(a, b, preferred_element_type=F32)


def _dot_nt(a, b):
    return lax.dot_general(a, b, (((1,), (1,)), ((), ())), preferred_element_type=F32)


def _norm_mod(x, g, sc, sh):
    ms = jnp.mean(x * x, axis=-1, keepdims=True)
    y = x * lax.rsqrt(ms + EPS)
    return (y * g) * (1.0 + sc) + sh


def _ada_kernel(c_ref, w_ref, b_ref, o_ref):
    c = c_ref[...]
    a = c * jax.nn.sigmoid(c)
    ah, al = _split_bf16(a)
    wh, wl = _split_bf16(w_ref[...])
    o_ref[...] = _dot(ah, wh) + _dot(al, wh) + _dot(ah, wl) + b_ref[...]


def _ada_mod(c, ada_w, ada_b):
    depth, d, d6 = ada_w.shape
    b = c.shape[0]
    rows = 8
    c_pad = jnp.zeros((rows, d), F32).at[:b].set(c)
    tn = 768
    out = pl.pallas_call(
        _ada_kernel,
        grid=(depth, d6 // tn),
        in_specs=[
            pl.BlockSpec((rows, d), lambda l, j: (0, 0)),
            pl.BlockSpec((None, d, tn), lambda l, j: (l, 0, j)),
            pl.BlockSpec((None, 1, tn), lambda l, j: (l, 0, j)),
        ],
        out_specs=pl.BlockSpec((None, rows, tn), lambda l, j: (l, 0, j)),
        out_shape=jax.ShapeDtypeStruct((depth, rows, d6), F32),
        compiler_params=_cparams("parallel", "parallel"),
        name="ada_mod",
    )(c_pad, ada_w, ada_b.reshape(depth, 1, d6))
    return out[:, :b, :]


def _inproj_kernel(x_ref, g_ref, sc_ref, sh_ref, w_ref, qg_ref, kg_ref, hm_ref, o_ref):
    h = _norm_mod(x_ref[...], g_ref[...], sc_ref[...], sh_ref[...])
    proj = _dot(h.astype(BF16), w_ref[...])
    hm = hm_ref[...]

    def head_norm(z, gain, scale):
        sh_, sl_ = _split_bf16(z * z)
        ss = _dot(sh_, hm) + _dot(sl_, hm)
        return z * lax.rsqrt(ss * (1.0 / HEAD_DIM) + EPS) * (gain * scale)

    o_ref[:, 0:D_ATTN] = head_norm(proj[:, 0:D_ATTN], qg_ref[...], HEAD_DIM ** -0.5)
    o_ref[:, D_ATTN:2 * D_ATTN] = head_norm(proj[:, D_ATTN:2 * D_ATTN], kg_ref[...], 1.0)
    o_ref[:, 2 * D_ATTN:] = proj[:, 2 * D_ATTN:]


def _inproj(x2, g, sc, sh, w_bf16, q_gain, k_gain, seq, tm):
    n, d = x2.shape
    dout = w_bf16.shape[1]
    tiles_per_seq = seq // tm
    head_of_lane = jnp.arange(D_ATTN) // HEAD_DIM
    hm = (head_of_lane[:, None] == head_of_lane[None, :]).astype(BF16)
    bmap = lambda i: (i // tiles_per_seq, 0, 0)
    return pl.pallas_call(
        _inproj_kernel,
        grid=(n // tm,),
        in_specs=[
            pl.BlockSpec((tm, d), lambda i: (i, 0)),
            pl.BlockSpec((1, d), lambda i: (0, 0)),
            pl.BlockSpec((None, 1, d), bmap),
            pl.BlockSpec((None, 1, d), bmap),
            pl.BlockSpec((d, dout), lambda i: (0, 0)),
            pl.BlockSpec((1, D_ATTN), lambda i: (0, 0)),
            pl.BlockSpec((1, D_ATTN), lambda i: (0, 0)),
            pl.BlockSpec((D_ATTN, D_ATTN), lambda i: (0, 0)),
        ],
        out_specs=pl.BlockSpec((tm, dout), lambda i: (i, 0)),
        out_shape=jax.ShapeDtypeStruct((n, dout), F32),
        compiler_params=_cparams("parallel"),
        name="even_inproj",
    )(x2, g, sc, sh, w_bf16, q_gain, k_gain, hm)


def _attn_kernel(q_ref, kp_ref, kc_ref, vp_ref, vc_ref, o_ref, kfull, vfull, o_s, lse_s):
    w = ATTN_WINDOW_STEPS
    first = pl.program_id(2) == 0
    kfull[0:SUPER, :] = kp_ref[...]
    kfull[SUPER:2 * SUPER, :] = kc_ref[...]
    vfull[0:SUPER, :] = vp_ref[...]
    vfull[SUPER:2 * SUPER, :] = vc_ref[...]

    qi = lax.broadcasted_iota(jnp.int32, (w, 2 * w), 0)
    kj = lax.broadcasted_iota(jnp.int32, (w, 2 * w), 1)
    delta = qi + w - kj
    band = (delta >= 0) & (delta <= w)
    lane = lax.broadcasted_iota(jnp.int32, (w, LANES), 1)
    head0 = lane < HEAD_DIM

    def sub_block(br, dil, qstart, kstart):
        qs = q_ref[pl.ds(qstart, w, stride=dil), :]
        ks = kfull[pl.ds(kstart, 2 * w, stride=dil), :].astype(BF16)
        vs = vfull[pl.ds(kstart, 2 * w, stride=dil), :].astype(BF16)
        valid = band & ((kstart + dil * kj >= SUPER) | jnp.logical_not(first))
        outs, lses = [], []
        for hd in range(2):
            qm = jnp.where(head0 if hd == 0 else jnp.logical_not(head0), qs, 0.0).astype(BF16)
            s = _dot_nt(qm, ks)
            s = jnp.where(valid, s, -jnp.inf)
            m = jnp.max(s, axis=1, keepdims=True)
            p = jnp.exp(s - m)
            den = jnp.sum(p, axis=1, keepdims=True)
            outs.append(_dot(p.astype(BF16), vs) / den)
            lses.append(m + jnp.log(den))
        o_s[br, pl.ds(qstart, w, stride=dil), :] = jnp.where(head0, outs[0], outs[1])
        lse_s[br, pl.ds(qstart, w, stride=dil), :] = jnp.where(head0, lses[0], lses[1])

    for br, dil in enumerate(DILATIONS):
        span = w * dil

        def body(it, carry, br=br, dil=dil, span=span):
            c = it // dil
            r = it % dil
            sub_block(br, dil, c * span + r, SUPER + (c - 1) * span + r)
            return carry

        lax.fori_loop(0, SUPER // w, body, 0)

    l0, l1, l2 = lse_s[0], lse_s[1], lse_s[2]
    m = jnp.maximum(jnp.maximum(l0, l1), l2)
    e0, e1, e2 = jnp.exp(l0 - m), jnp.exp(l1 - m), jnp.exp(l2 - m)
    o_ref[...] = (e0 * o_s[0] + e1 * o_s[1] + e2 * o_s[2]) / (e0 + e1 + e2)


def _attention(proj, batch, seq):
    n = proj.shape[0]
    nsb = seq // SUPER
    hp = D_ATTN // LANES
    blk = (SUPER, LANES)
    cur = lambda off: (lambda b, h, s: (b * nsb + s, off + h))
    prev = lambda off: (lambda b, h, s: (b * nsb + jnp.maximum(s - 1, 0), off + h))
    return pl.pallas_call(
        _attn_kernel,
        grid=(batch, hp, nsb),
        in_specs=[
            pl.BlockSpec(blk, cur(0)),
            pl.BlockSpec(blk, prev(hp)),
            pl.BlockSpec(blk, cur(hp)),
            pl.BlockSpec(blk, prev(2 * hp)),
            pl.BlockSpec(blk, cur(2 * hp)),
        ],
        out_specs=pl.BlockSpec(blk, cur(0)),
        out_shape=jax.ShapeDtypeStruct((n, D_ATTN), F32),
        scratch_shapes=[
            pltpu.VMEM((2 * SUPER, LANES), F32),
            pltpu.VMEM((2 * SUPER, LANES), F32),
            pltpu.VMEM((len(DILATIONS), SUPER, LANES), F32),
            pltpu.VMEM((len(DILATIONS), SUPER, LANES), F32),
        ],
        compiler_params=_cparams("parallel", "parallel", "arbitrary"),
        name="dilated_attn",
    )(proj, proj, proj, proj, proj)


def _outproj_kernel(x_ref, ya_ref, u_ref, pw_ref, ps_ref, w_ref, g1_ref, o_ref, halo, ext,
                    *, tiles_per_seq):
    tm = x_ref.shape[0]
    ti = pl.program_id(0) % tiles_per_seq
    u = u_ref[...]

    @pl.when(ti == 0)
    def _():
        ext[0:POOL_HALO, :] = jnp.zeros((POOL_HALO, u.shape[1]), F32)

    @pl.when(ti != 0)
    def _():
        ext[0:POOL_HALO, :] = halo[...]

    ext[POOL_HALO:, :] = u
    halo[...] = u[tm - POOL_HALO:, :]

    pos1 = (ti * tm + 1 + lax.broadcasted_iota(jnp.int32, (tm, POOL_GROUP), 0)).astype(F32)
    ypool = []
    for gi, win in enumerate(POOL_WINDOWS):
        lo, hi = gi * POOL_GROUP, (gi + 1) * POOL_GROUP
        acc = ext[:, lo:hi]
        step = 1
        while step < win:
            acc = acc + pltpu.roll(acc, step, 0)
            step *= 2
        pooled = acc[POOL_HALO:, :] / jnp.minimum(pos1, float(win))
        diff = pooled - u[:, lo:hi]
        ypool.append(_dot(diff.astype(BF16), pw_ref[gi]) * ps_ref[:, lo:hi])
    ypool = jnp.concatenate(ypool, axis=1)
    dm = ya_ref.shape[1]
    y = _dot(ya_ref[...].astype(BF16), w_ref[0:dm, :]) + _dot(ypool.astype(BF16), w_ref[dm:, :])
    o_ref[...] = x_ref[...] + g1_ref[...] * y


def _outproj(x2, y_attn, proj, pool_w_bf16, pool_scale, w_bf16, g1, seq, tm):
    n, d = x2.shape
    dp = pool_scale.shape[1]
    tiles_per_seq = seq // tm
    bmap = lambda i: (i // tiles_per_seq, 0, 0)
    ucol = proj.shape[1] // dp - 1
    return pl.pallas_call(
        functools.partial(_outproj_kernel, tiles_per_seq=tiles_per_seq),
        grid=(n // tm,),
        in_specs=[
            pl.BlockSpec((tm, d), lambda i: (i, 0)),
            pl.BlockSpec((tm, D_ATTN), lambda i: (i, 0)),
            pl.BlockSpec((tm, dp), lambda i: (i, ucol)),
            pl.BlockSpec(pool_w_bf16.shape, lambda i: (0, 0, 0)),
            pl.BlockSpec((1, dp), lambda i: (0, 0)),
            pl.BlockSpec(w_bf16.shape, lambda i: (0, 0)),
            pl.BlockSpec((None, 1, d), bmap),
        ],
        out_specs=pl.BlockSpec((tm, d), lambda i: (i, 0)),
        out_shape=jax.ShapeDtypeStruct((n, d), F32),
        scratch_shapes=[
            pltpu.VMEM((POOL_HALO, dp), F32),
            pltpu.VMEM((tm + POOL_HALO, dp), F32),
        ],
        compiler_params=_cparams("arbitrary"),
        name="even_outproj",
    )(x2, y_attn, proj, pool_w_bf16, pool_scale, w_bf16, g1)


def _conv_kernel(x_ref, g_ref, sc_ref, sh_ref, win_ref, cw_ref, wout_ref, g1_ref, o_ref, carry, ext,
                 *, tiles_per_seq):
    tm, d = x_ref.shape
    ti = pl.program_id(0) % tiles_per_seq
    x = x_ref[...]
    h = _norm_mod(x, g_ref[...], sc_ref[...], sh_ref[...])
    p = _dot(h.astype(BF16), win_ref[...])
    u = p[:, 2 * d:] * p[:, 0:d]

    @pl.when(ti == 0)
    def _():
        ext[0:CONV_HALO, :] = jnp.zeros((CONV_HALO, d), F32)

    @pl.when(ti != 0)
    def _():
        ext[0:CONV_HALO, :] = carry[...]

    ext[CONV_HALO:, :] = u
    carry[...] = u[tm - CONV_HALO:, :]
    y = cw_ref[0:1, :] * u
    for lag in range(1, cw_ref.shape[0]):
        y = y + cw_ref[lag:lag + 1, :] * ext[pl.ds(CONV_HALO - lag, tm), :]
    z = p[:, d:2 * d] * y
    o_ref[...] = x + g1_ref[...] * _dot(z.astype(BF16), wout_ref[...])


def _conv_layer(x2, g, sc, sh, win_bf16, conv_w, wout_bf16, g1, seq, tm):
    n, d = x2.shape
    tiles_per_seq = seq // tm
    bmap = lambda i: (i // tiles_per_seq, 0, 0)
    return pl.pallas_call(
        functools.partial(_conv_kernel, tiles_per_seq=tiles_per_seq),
        grid=(n // tm,),
        in_specs=[
            pl.BlockSpec((tm, d), lambda i: (i, 0)),
            pl.BlockSpec((1, d), lambda i: (0, 0)),
            pl.BlockSpec((None, 1, d), bmap),
            pl.BlockSpec((None, 1, d), bmap),
            pl.BlockSpec(win_bf16.shape, lambda i: (0, 0)),
            pl.BlockSpec(conv_w.shape, lambda i: (0, 0)),
            pl.BlockSpec(wout_bf16.shape, lambda i: (0, 0)),
            pl.BlockSpec((None, 1, d), bmap),
        ],
        out_specs=pl.BlockSpec((tm, d), lambda i: (i, 0)),
        out_shape=jax.ShapeDtypeStruct((n, d), F32),
        scratch_shapes=[
            pltpu.VMEM((CONV_HALO, d), F32),
            pltpu.VMEM((tm + CONV_HALO, d), F32),
        ],
        compiler_params=_cparams("arbitrary"),
        name="conv_mixer",
    )(x2, g, sc, sh, win_bf16, conv_w, wout_bf16, g1)


PEER_CAND_ROWS = 80
ROUTE_LANES = 256


def _topk_rows(ss, key, k):
    t = ss[0].shape[1]
    kiota = lax.broadcasted_iota(jnp.int32, (k, t), 0)
    ss = list(ss)
    vals = [jnp.zeros((k, t), F32) for _ in ss]
    keys = [jnp.zeros((k, t), F32) for _ in ss]
    for j in range(k):
        for n, s in enumerate(ss):
            m = jnp.max(s, axis=0, keepdims=True)
            i = jnp.min(jnp.where(s == m, key, 1e9), axis=0, keepdims=True)
            vals[n] = jnp.where(kiota == j, m, vals[n])
            keys[n] = jnp.where(kiota == j, i, keys[n])
            ss[n] = jnp.where(key == i, -jnp.inf, s)
    return list(zip(vals, keys))


def _route_kernel(x_ref, g_ref, sc_ref, sh_ref, wh_ref, wl_ref, k1h_ref, k1l_ref, k2h_ref, k2l_ref,
                  h_ref, ids_ref, gate_ref, q_s, ids_s, gate_s):
    tr = x_ref.shape[0]
    tw = min(tr, ROUTE_LANES)
    nhalf = tr // tw
    k = PEER_TOPK
    h = _norm_mod(x_ref[...], g_ref[...], sc_ref[...], sh_ref[...])
    h_ref[...] = h
    hh, hl = _split_bf16(h)
    wh = wh_ref[...]
    q = _dot(hh, wh) + _dot(hl, wh) + _dot(hh, wl_ref[...])
    for p in range(2 * PEER_HEADS):
        q_s[p] = q[:, p * PEER_HALF:(p + 1) * PEER_HALF]

    def scores(kh_ref, kl_ref, qpart):
        qh, ql = _split_bf16(qpart)
        kh = kh_ref[...]
        return _dot_nt(kh, qh) + _dot_nt(kh, ql) + _dot_nt(kl_ref[...], qh)

    def body(it, carry):
        hd = it // nhalf
        half = it % nhalf
        r0 = pl.multiple_of(half * tw, tw)
        key_rows = lax.broadcasted_iota(jnp.int32, (PEER_KEYS, tw), 0).astype(F32)
        ((v1, i1),) = _topk_rows([scores(k1h_ref, k1l_ref, q_s[2 * hd, pl.ds(r0, tw), :])], key_rows, k)
        ((v2, i2),) = _topk_rows([scores(k2h_ref, k2l_ref, q_s[2 * hd + 1, pl.ds(r0, tw), :])], key_rows, k)
        i1 = i1.astype(jnp.int32)
        i2 = i2.astype(jnp.int32)

        r = lax.broadcasted_iota(jnp.int32, (PEER_CAND_ROWS, tw), 0)
        blk, j8 = r >> 3, r & 7
        flat = jnp.where(blk <= 1, r, jnp.where(blk <= 8, k * (blk - 1) + j8, k * (8 + j8)))
        jlim = jnp.where(blk == 3, 5, jnp.where(blk == 4, 4, jnp.where(blk == 5, 3, 2)))
        valid = (blk <= 2) | (blk == 9) | (j8 < jlim)
        cand = jnp.concatenate(
            [v1[0:1, :] + v2]
            + [v1[i:i + 1, :] + v2[0:8, :] for i in range(1, 8)]
            + [v1[8:16, :] + v2[0:1, :]], axis=0)
        cand = jnp.where(valid, cand, -jnp.inf)
        ((vs, cf),) = _topk_rows([cand], flat.astype(F32), k)
        cf = cf.astype(jnp.int32)

        kiota = lax.broadcasted_iota(jnp.int32, (k, tw), 0)
        ids = jnp.zeros((k, tw), jnp.int32)
        for j in range(k):
            c = cf[j:j + 1, :]
            e1 = jnp.sum(jnp.where(kiota == (c >> 4), i1, 0), axis=0, keepdims=True)
            e2 = jnp.sum(jnp.where(kiota == (c & (k - 1)), i2, 0), axis=0, keepdims=True)
            ids = jnp.where(kiota == j, e1 * PEER_KEYS + e2, ids)
        p = jnp.exp(vs - vs[0:1, :])
        row0 = pl.multiple_of(hd * k, k)
        ids_s[half, pl.ds(row0, k), :] = ids
        gate_s[half, pl.ds(row0, k), :] = p / jnp.sum(p, axis=0, keepdims=True)
        return carry

    lax.fori_loop(0, PEER_HEADS * nhalf, body, 0)
    for half in range(nhalf):
        ids_ref[half * tw:(half + 1) * tw, :] = ids_s[half].T
        gate_ref[half * tw:(half + 1) * tw, :] = gate_s[half].T


def _route(x2, g, sc, sh, wq_hi, wq_lo, k1, k2, seq, tr):
    n, d = x2.shape
    tw = min(tr, ROUTE_LANES)
    assert PEER_TOPK == 16 and tr % tw == 0 and tw % LANES == 0
    tiles_per_seq = seq // tr
    bmap = lambda i: (i // tiles_per_seq, 0, 0)
    k1h, k1l = _split_bf16(k1)
    k2h, k2l = _split_bf16(k2)
    full2 = lambda a: pl.BlockSpec(a.shape, lambda i: (0, 0))
    return pl.pallas_call(
        _route_kernel,
        grid=(n // tr,),
        in_specs=[
            pl.BlockSpec((tr, d), lambda i: (i, 0)),
            pl.BlockSpec((1, d), lambda i: (0, 0)),
            pl.BlockSpec((None, 1, d), bmap),
            pl.BlockSpec((None, 1, d), bmap),
            full2(wq_hi), full2(wq_lo), full2(k1h), full2(k1l), full2(k2h), full2(k2l),
        ],
        out_specs=[
            pl.BlockSpec((tr, d), lambda i: (i, 0)),
            pl.BlockSpec((tr, PEER_SLOTS), lambda i: (i, 0)),
            pl.BlockSpec((tr, PEER_SLOTS), lambda i: (i, 0)),
        ],
        out_shape=[
            jax.ShapeDtypeStruct((n, d), F32),
            jax.ShapeDtypeStruct((n, PEER_SLOTS), jnp.int32),
            jax.ShapeDtypeStruct((n, PEER_SLOTS), F32),
        ],
        scratch_shapes=[
            pltpu.VMEM((2 * PEER_HEADS, tr, PEER_HALF), F32),
            pltpu.VMEM((tr // tw, PEER_SLOTS, tw), jnp.int32),
            pltpu.VMEM((tr // tw, PEER_SLOTS, tw), F32),
        ],
        compiler_params=_cparams("parallel"),
        name="peer_route",
    )(x2, g, sc, sh, wq_hi, wq_lo, k1h, k1l, k2h, k2l)


PEER_BUFS = 8


def _gather_kernel(ids_hbm, gate_ref, h_ref, x_ref, g2_ref, uv_hbm, o_ref, ids_smem, *scratch):
    bufs, sems, ids_sem = scratch[:PEER_BUFS], scratch[PEER_BUFS], scratch[PEER_BUFS + 1]
    tg, d = x_ref.shape
    step = pl.program_id(0)
    ids_copy = pltpu.make_async_copy(
        ids_hbm.at[pl.ds(step * tg * PEER_SLOTS, tg * PEER_SLOTS)], ids_smem, ids_sem)
    ids_copy.start()
    ids_copy.wait()

    def issue(t, slot):
        for a in range(PEER_SLOTS):
            e = ids_smem[t * PEER_SLOTS + a]
            pltpu.make_async_copy(uv_hbm.at[pl.ds(e, 1), :], bufs[slot].at[pl.ds(a, 1), :], sems.at[slot]).start()

    def wait(slot):
        pltpu.make_async_copy(uv_hbm.at[pl.ds(0, PEER_SLOTS), :], bufs[slot], sems.at[slot]).wait()

    eye = (lax.broadcasted_iota(jnp.int32, (PEER_SLOTS, PEER_SLOTS), 0)
           == lax.broadcasted_iota(jnp.int32, (PEER_SLOTS, PEER_SLOTS), 1))
    g2 = g2_ref[...]

    def compute(t, slot):
        buf = bufs[slot]
        hrow = h_ref[pl.ds(t, 1), :]
        act = jnp.sum(buf[:, 0:d] * hrow, axis=1, keepdims=True)
        gelu = 0.5 * act * (1.0 + lax.erf(act * (1.0 / math.sqrt(2.0))))
        gcol = jnp.sum(jnp.where(eye, gate_ref[pl.ds(t, 1), :], 0.0), axis=1, keepdims=True)
        y = jnp.sum(buf[:, d:2 * d] * (gcol * gelu), axis=0, keepdims=True)
        o_ref[pl.ds(t, 1), :] = x_ref[pl.ds(t, 1), :] + g2 * y

    ahead = PEER_BUFS - 1
    for s in range(ahead):
        issue(s, s)

    def token(t, s, refill):
        wait(s)
        if refill:
            issue(t + ahead, (s + ahead) % PEER_BUFS)
        compute(t, s)

    def group(j, carry):
        for s in range(PEER_BUFS):
            token(j * PEER_BUFS + s, s, True)
        return carry

    n_groups = tg // PEER_BUFS
    lax.fori_loop(0, n_groups - 1, group, 0)
    for s in range(PEER_BUFS):
        t = (n_groups - 1) * PEER_BUFS + s
        token(t, s, t + ahead < tg)


def _gather_mix(ids, gates, h2, x2, g2, uv, seq, tg):
    n, d = x2.shape
    assert tg % PEER_BUFS == 0 and tg >= 2 * PEER_BUFS
    tiles_per_seq = seq // tg
    bmap = lambda i: (i // tiles_per_seq, 0, 0)
    return pl.pallas_call(
        _gather_kernel,
        grid=(n // tg,),
        in_specs=[
            pl.BlockSpec(memory_space=pl.ANY),
            pl.BlockSpec((tg, PEER_SLOTS), lambda i: (i, 0)),
            pl.BlockSpec((tg, d), lambda i: (i, 0)),
            pl.BlockSpec((tg, d), lambda i: (i, 0)),
            pl.BlockSpec((None, 1, d), bmap),
            pl.BlockSpec(memory_space=pl.ANY),
        ],
        out_specs=pl.BlockSpec((tg, d), lambda i: (i, 0)),
        out_shape=jax.ShapeDtypeStruct((n, d), F32),
        scratch_shapes=(
            [pltpu.SMEM((tg * PEER_SLOTS,), jnp.int32)]
            + [pltpu.VMEM((PEER_SLOTS, 2 * d), F32) for _ in range(PEER_BUFS)]
            + [pltpu.SemaphoreType.DMA((PEER_BUFS,)), pltpu.SemaphoreType.DMA]
        ),
        compiler_params=_cparams("arbitrary"),
        name="peer_gather",
    )(ids.reshape(-1), gates, h2, x2, g2, uv)


def _pick_tile(seq, want):
    t = min(want, seq)
    while seq % t:
        t //= 2
    return t


def kernel(x, c, ada_w, ada_b, norm_mix_g, norm_ffn_g, ab_w_in, ab_q_gain, ab_k_gain, pool_w, pool_scale, ab_w_out, conv_w_in, conv_w, conv_w_out, peer_w_q, peer_key1, peer_key2, peer_u, peer_v):
    b, s, d = x.shape
    depth = ada_w.shape[0]
    assert s % SUPER == 0 and d % LANES == 0
    x2 = x.reshape(b * s, d)
    mods = _ada_mod(c, ada_w, ada_b).reshape(depth, b, 6, 1, d)

    tm_dense = _pick_tile(s, 512)
    tm_conv = _pick_tile(s, 256)
    tm_peer = _pick_tile(s, 256)

    for layer in range(depth):
        sh1, sc1, g1, sh2, sc2, g2 = [mods[layer, :, i] for i in range(6)]
        gm = norm_mix_g[layer].reshape(1, d)
        if layer % 2 == 0:
            e = layer // 2
            proj = _inproj(x2, gm, sc1, sh1, ab_w_in[e].astype(BF16),
                           ab_q_gain[e].reshape(1, D_ATTN), ab_k_gain[e].reshape(1, D_ATTN), s, tm_dense)
            y_attn = _attention(proj, b, s)
            x2 = _outproj(x2, y_attn, proj, pool_w[e].astype(BF16), pool_scale[e].reshape(1, -1),
                          ab_w_out[e].astype(BF16), g1, s, tm_dense)
        else:
            o = layer // 2
            x2 = _conv_layer(x2, gm, sc1, sh1, conv_w_in[o].astype(BF16), conv_w[o],
                             conv_w_out[o].astype(BF16), g1, s, tm_conv)
        wq_hi, wq_lo = _split_bf16(peer_w_q[layer])
        h2, ids, gates = _route(x2, norm_ffn_g[layer].reshape(1, d), sc2, sh2, wq_hi, wq_lo,
                                peer_key1[layer], peer_key2[layer], s, tm_peer)
        uv = jnp.concatenate([peer_u[layer], peer_v[layer]], axis=1)
        x2 = _gather_mix(ids, gates, h2, x2, g2, uv, s, tm_peer)
    return x2.reshape(b, s, d)
```
